```python
import jax, jax.numpy as jnp
from jax import lax
import numpy as np

D_MODEL = 1024
BATCH = 8
SEQ = 2048
DEPTH = 2

CHUNK = 64
Q_BLOCK = 128
N_A_LAYERS = (DEPTH + 1) // 2
N_B_LAYERS = DEPTH - N_A_LAYERS
A_HEADS = 16
A_HEAD_DIM = D_MODEL // A_HEADS
B_HEADS = 16
NOPE_DIM = 64
ROPE_DIM = 32
V_DIM = 64
Q_RANK = 768
KV_RANK = 256
ROPE_THETA = 10000.0
N_EXPERTS = 16
N_GROUPS = 4
EXPERTS_PER_GROUP = N_EXPERTS // N_GROUPS
GROUP_SCORE_TOP = 2
TOP_K = 2
D_EXPERT = 512
EPS = 1e-6

kernel_name = "fox_mla_yoco_grouped_moe_adaln"


def rmsnorm(x, g):
    xf = x.astype(jnp.float32)
    y = xf * lax.rsqrt(jnp.mean(xf * xf, axis=-1, keepdims=True) + EPS)
    return y.astype(x.dtype) * g


def modulate(h, shift, scale):
    return h * (1 + scale[:, None, :]) + shift[:, None, :]


def rope_tables(positions, dtype):
    half = ROPE_DIM // 2
    inv_freq = ROPE_THETA ** (-jnp.arange(half, dtype=jnp.float32) / half)
    ang = positions.astype(jnp.float32)[..., None] * inv_freq
    return jnp.cos(ang).astype(dtype), jnp.sin(ang).astype(dtype)


def apply_rope(x, cos, sin):
    x1, x2 = jnp.split(x, 2, axis=-1)
    return jnp.concatenate([x1 * cos - x2 * sin, x2 * cos + x1 * sin], axis=-1)


def block_sweep(attend, seq):
    return jnp.concatenate([attend(i * Q_BLOCK, (i + 1) * Q_BLOCK) for i in range(seq // Q_BLOCK)], axis=1)


def forgetting_attention(h, w_in, b_f, w_o):
    B, S, D = h.shape
    proj = h @ w_in
    q, k, v, f_logit = jnp.split(proj, [D, 2 * D, 3 * D], axis=-1)
    q = q.reshape(B, S, A_HEADS, A_HEAD_DIM)
    k = k.reshape(B, S, A_HEADS, A_HEAD_DIM)
    v = v.reshape(B, S, A_HEADS, A_HEAD_DIM)
    log_f = jax.nn.log_sigmoid((f_logit + b_f).astype(jnp.float32))
    cum = jnp.cumsum(log_f, axis=1).transpose(0, 2, 1)
    scale = A_HEAD_DIM ** -0.5
    pos = jnp.arange(S)

    def attend(t0, t1):
        s = jnp.einsum('bqhd,bkhd->bhqk', q[:, t0:t1], k[:, :t1], preferred_element_type=jnp.float32) * scale
        s = s + cum[:, :, t0:t1, None] - cum[:, :, None, :t1]
        allowed = pos[t0:t1, None] >= pos[None, :t1]
        p = jax.nn.softmax(jnp.where(allowed, s, -jnp.inf), axis=-1)
        return jnp.einsum('bhqk,bkhd->bqhd', p.astype(v.dtype), v[:, :t1])

    o = block_sweep(attend, S).reshape(B, S, D)
    return o @ w_o


def mla_shared_kv(x, c_act, norm_g, w_mod, b_mod, w_down, latent_g, w_up, cos, sin):
    B, S, _ = x.shape
    shift, scale = jnp.split(c_act @ w_mod + b_mod, 2, axis=-1)
    h = modulate(rmsnorm(x, norm_g), shift, scale)
    c_kv, k_rope = jnp.split(h @ w_down, [KV_RANK], axis=-1)
    c_kv = rmsnorm(c_kv, latent_g)
    kv = (c_kv @ w_up).reshape(B, S, B_HEADS, NOPE_DIM + V_DIM)
    k_nope, v = jnp.split(kv, [NOPE_DIM], axis=-1)
    k_rope = apply_rope(k_rope, cos, sin)
    return k_nope, k_rope, v


def latent_attention(h, k_nope, k_rope, v, w_dq, q_g, w_uq, w_o, cos, sin):
    B, S, _ = h.shape
    cq = rmsnorm(h @ w_dq, q_g)
    q = (cq @ w_uq).reshape(B, S, B_HEADS, NOPE_DIM + ROPE_DIM)
    q_nope, q_rope = jnp.split(q, [NOPE_DIM], axis=-1)
    q_rope = apply_rope(q_rope, cos[:, :, None, :], sin[:, :, None, :])
    scale = (NOPE_DIM + ROPE_DIM) ** -0.5
    chunk_id = jnp.arange(S) // CHUNK

    def attend(t0, t1):
        s = jnp.einsum('bqhd,bkhd->bhqk', q_nope[:, t0:t1], k_nope[:, :t1], preferred_element_type=jnp.float32)
        s = s + jnp.einsum('bqhr,bkr->bhqk', q_rope[:, t0:t1], k_rope[:, :t1], preferred_element_type=jnp.float32)
        allowed = chunk_id[t0:t1, None] >= chunk_id[None, :t1]
        p = jax.nn.softmax(jnp.where(allowed, s * scale, -jnp.inf), axis=-1)
        return jnp.einsum('bhqk,bkhd->bqhd', p.astype(v.dtype), v[:, :t1])

    o = block_sweep(attend, S).reshape(B, S, B_HEADS * V_DIM)
    return o @ w_o


def grouped_moe(h, router_w, router_bias, w_gate, w_up, w_down):
    B, S, D = h.shape
    hf = h.reshape(-1, D)
    n = hf.shape[0]
    scores = jax.nn.sigmoid(jnp.matmul(hf, router_w, preferred_element_type=jnp.float32))
    sel = (scores + router_bias.astype(jnp.float32)).reshape(n, N_GROUPS, EXPERTS_PER_GROUP)
    group_score = lax.top_k(sel, GROUP_SCORE_TOP)[0].sum(-1)
    g = jnp.argmax(group_score, axis=-1)
    g_idx = jnp.broadcast_to(g[:, None, None], (n, 1, EXPERTS_PER_GROUP))
    in_group = jnp.take_along_axis(sel, g_idx, axis=1)[:, 0]
    _, local = lax.top_k(in_group, TOP_K)
    idx = g[:, None] * EXPERTS_PER_GROUP + local
    w = jnp.take_along_axis(scores, idx, axis=1)
    w = w / jnp.sum(w, axis=-1, keepdims=True)
    combine = jnp.sum(jax.nn.one_hot(idx, N_EXPERTS, dtype=jnp.float32) * w[..., None], axis=1).astype(h.dtype)
    y = jnp.zeros_like(hf)
    for e in range(N_EXPERTS):
        a = jax.nn.silu(hf @ w_gate[e]) * (hf @ w_up[e])
        y = y + combine[:, e:e + 1] * (a @ w_down[e])
    return y.reshape(B, S, D)


def setup_inputs(seed: int = 0) -> dict:
    key = jax.random.key(seed)
    ks = iter(jax.random.split(key, 40))
    f32 = jnp.float32
    D = D_MODEL

    def w(shape, fan_in, gain=1.0):
        return gain * fan_in ** -0.5 * jax.random.normal(next(ks), shape, f32)

    def gain(shape):
        return 1.0 + 0.05 * jax.random.normal(next(ks), shape, f32)

    def small(shape, s):
        return s * jax.random.normal(next(ks), shape, f32)

    x = jax.random.normal(next(ks), (BATCH, SEQ, D), f32)
    c = jax.random.normal(next(ks), (BATCH, D), f32)
    offset = jax.random.randint(next(ks), (BATCH, 1), 0, 4096, dtype=jnp.int32)
    positions = (jnp.arange(SEQ, dtype=jnp.int32)[None, :] + offset).astype(jnp.int32)
    return {
        "x": x,
        "c": c,
        "positions": positions,
        "a_norm_g": gain((N_A_LAYERS, D)),
        "a_w_in": w((N_A_LAYERS, D, 3 * D + A_HEADS), D),
        "a_b_f": jax.random.uniform(next(ks), (N_A_LAYERS, A_HEADS), f32, 1.0, 5.0),
        "a_w_o": w((N_A_LAYERS, D, D), D),
        "kv_norm_g": gain((D,)),
        "kv_w_mod": w((D, 2 * D), D, 0.5),
        "kv_b_mod": small((2 * D,), 0.02),
        "kv_w_down": w((D, KV_RANK + ROPE_DIM), D),
        "kv_latent_g": gain((KV_RANK,)),
        "kv_w_up": w((KV_RANK, B_HEADS * (NOPE_DIM + V_DIM)), KV_RANK),
        "b_norm_g": gain((N_B_LAYERS, D)),
        "b_w_dq": w((N_B_LAYERS, D, Q_RANK), D),
        "b_q_norm_g": gain((N_B_LAYERS, Q_RANK)),
        "b_w_uq": w((N_B_LAYERS, Q_RANK, B_HEADS * (NOPE_DIM + ROPE_DIM)), Q_RANK),
        "b_w_o": w((N_B_LAYERS, B_HEADS * V_DIM, D), B_HEADS * V_DIM),
        "w_mod": w((DEPTH, D, 6 * D), D, 0.5),
        "b_mod": small((DEPTH, 6 * D), 0.02),
        "ffn_norm_g": gain((DEPTH, D)),
        "router_w": w((D, N_EXPERTS), D),
        "router_bias": small((N_EXPERTS,), 0.01),
        "exp_w_gate": w((DEPTH, N_EXPERTS, D, D_EXPERT), D),
        "exp_w_up": w((DEPTH, N_EXPERTS, D, D_EXPERT), D),
        "exp_w_down": w((DEPTH, N_EXPERTS, D_EXPERT, D), D_EXPERT),
        "final_norm_g": gain((D,)),
    }


def reference(x, c, positions, a_norm_g, a_w_in, a_b_f, a_w_o, kv_norm_g, kv_w_mod, kv_b_mod,
              kv_w_down, kv_latent_g, kv_w_up, b_norm_g, b_w_dq, b_q_norm_g, b_w_uq, b_w_o,
              w_mod, b_mod, ffn_norm_g, router_w, router_bias, exp_w_gate, exp_w_up, exp_w_down,
              final_norm_g):
    c_act = jax.nn.silu(c)
    cos, sin = rope_tables(positions, x.dtype)
    shared_kv = None
    for layer in range(DEPTH):
        sh1, sc1, g1, sh2, sc2, g2 = jnp.split(c_act @ w_mod[layer] + b_mod[layer], 6, axis=-1)
        if layer < N_A_LAYERS:
            h = modulate(rmsnorm(x, a_norm_g[layer]), sh1, sc1)
            mix = forgetting_attention(h, a_w_in[layer], a_b_f[layer], a_w_o[layer])
        else:
            j = layer - N_A_LAYERS
            h = modulate(rmsnorm(x, b_norm_g[j]), sh1, sc1)
            k_nope, k_rope, v = shared_kv
            mix = latent_attention(h, k_nope, k_rope, v, b_w_dq[j], b_q_norm_g[j], b_w_uq[j], b_w_o[j], cos, sin)
        x = x + g1[:, None, :] * mix
        h = modulate(rmsnorm(x, ffn_norm_g[layer]), sh2, sc2)
        x = x + g2[:, None, :] * grouped_moe(h, router_w, router_bias, exp_w_gate[layer], exp_w_up[layer], exp_w_down[layer])
        if layer == N_A_LAYERS - 1:
            shared_kv = mla_shared_kv(x, c_act, kv_norm_g, kv_w_mod, kv_b_mod, kv_w_down, kv_latent_g, kv_w_up, cos, sin)
    return rmsnorm(x, final_norm_g)
```

```python
import functools

import jax
import jax.numpy as jnp
from jax import lax
from jax.experimental import pallas as pl
from jax.experimental.pallas import tpu as pltpu

F32 = jnp.float32
BF16 = jnp.bfloat16
I32 = jnp.int32
U32 = jnp.uint32

D_MODEL = 1024
N_HEADS = 16
HEAD_DIM = 64
NOPE_DIM = 64
ROPE_DIM = 32
V_DIM = 64
Q_RANK = 768
KV_RANK = 256
ROPE_THETA = 10000.0
CHUNK = 64
N_EXPERTS = 16
N_GROUPS = 4
EXPERTS_PER_GROUP = 4
D_EXPERT = 512
EPS = 1e-6

LANES = 128
HEAD_PAD = LANES
N_PAIRS = 6
N_CLASSES = N_GROUPS * N_PAIRS
CLASS_PAD = 32
ROW_TILE = 256
ATTN_TILE = 256
MOE_TILE = 256
RANK_TILE = 512
PACK_W = D_MODEL + 2 * LANES
VMEM_LIMIT = 56 * 1024 * 1024


def _params(sem):
    return pltpu.CompilerParams(dimension_semantics=sem, vmem_limit_bytes=VMEM_LIMIT)


def _const_spec(shape):
    zeros = (0,) * len(shape)
    return pl.BlockSpec(shape, lambda *_: zeros)


def _norm_mod(x, g, shift, scale):
    y = x * lax.rsqrt(jnp.mean(x * x, axis=-1, keepdims=True) + EPS) * g
    return y * (1.0 + scale) + shift


def _split3(x):
    a = x.astype(BF16).astype(F32)
    r = x - a
    b = r.astype(BF16).astype(F32)
    c = (r - b).astype(BF16).astype(F32)
    return a, b, c


def _dot(a, b):
    return jnp.dot(a, b, preferred_element_type=F32)


def _dot_nt(a, b):
    return lax.dot_general(a, b, (((1,), (1,)), ((), ())), preferred_element_type=F32)


def _mod_kernel(c_ref, w_ref, b_ref, o_ref):
    c = c_ref[...]
    ca = c * jax.nn.sigmoid(c)
    o_ref[...] = _dot(ca.astype(BF16), w_ref[...].astype(BF16)) + b_ref[...]


def _mod_call(c, w, b):
    bsz, d = c.shape
    n = w.shape[1]
    tn = 1024
    return pl.pallas_call(
        _mod_kernel,
        grid=(n // tn,),
        in_specs=[
            pl.BlockSpec((bsz, d), lambda j: (0, 0)),
            pl.BlockSpec((d, tn), lambda j: (0, j)),
            pl.BlockSpec((1, tn), lambda j: (0, j)),
        ],
        out_specs=pl.BlockSpec((bsz, tn), lambda j: (0, j)),
        out_shape=jax.ShapeDtypeStruct((bsz, n), F32),
        compiler_params=_params(("arbitrary",)),
        name="mod",
    )(c, w, b.reshape(1, n))


def _fox_proj_kernel(x_ref, mod_ref, g_ref, wf_ref, bf_ref, tril_ref, pq_ref, pk_ref,
                     wq_ref, wk_ref, wv_ref, q_ref, k_ref, v_ref, carry_ref):
    @pl.when(pl.program_id(1) == 0)
    def _():
        carry_ref[...] = jnp.zeros_like(carry_ref)

    h = _norm_mod(x_ref[0], g_ref[...], mod_ref[0, 0:1, :], mod_ref[0, 1:2, :])
    hb = h.astype(BF16)

    z = _dot(hb, wf_ref[...]) + bf_ref[...]
    logf = jnp.minimum(z, 0.0) - jnp.log(1.0 + jnp.exp(-jnp.abs(z)))
    tril = tril_ref[...]
    a, b, c = _split3(logf)
    cum = carry_ref[...] + (_dot(tril, a.astype(BF16)) + _dot(tril, b.astype(BF16))
                            + _dot(tril, c.astype(BF16)))
    carry_ref[...] = cum[-1:, :]

    hi, mid, lo = _split3(cum)
    lane = lax.broadcasted_iota(I32, cum.shape, 1)
    e = jnp.where(lane < N_HEADS, hi,
                  jnp.where(lane < 2 * N_HEADS, mid,
                            jnp.where(lane < 3 * N_HEADS, lo,
                                      jnp.where(lane == 3 * N_HEADS, 1.0, 0.0)))).astype(BF16)
    scale = HEAD_DIM ** -0.5
    q_ref[0] = (_dot(hb, wq_ref[...]) * scale + _dot(e, pq_ref[...])).astype(BF16)
    k_ref[0] = (_dot(hb, wk_ref[...]) + _dot(e, pk_ref[...])).astype(BF16)
    v_ref[0] = _dot(hb, wv_ref[...]).astype(BF16)


def _fox_proj_call(x, mod, g, wf, bf, tril, pq, pk, wq, wk, wv):
    bsz, s, d = x.shape
    tm = ROW_TILE
    hp = N_HEADS * HEAD_PAD
    row = lambda w: pl.BlockSpec((1, tm, w), lambda b, i: (b, i, 0))
    return pl.pallas_call(
        _fox_proj_kernel,
        grid=(bsz, s // tm),
        in_specs=[
            row(d),
            pl.BlockSpec((1, 6, d), lambda b, i: (b, 0, 0)),
            _const_spec((1, d)),
            _const_spec(wf.shape), _const_spec(bf.shape), _const_spec(tril.shape),
            _const_spec(pq.shape), _const_spec(pk.shape),
            _const_spec(wq.shape), _const_spec(wk.shape), _const_spec(wv.shape),
        ],
        out_specs=[row(hp), row(hp), row(d)],
        out_shape=[jax.ShapeDtypeStruct((bsz, s, hp), BF16),
                   jax.ShapeDtypeStruct((bsz, s, hp), BF16),
                   jax.ShapeDtypeStruct((bsz, s, d), BF16)],
        scratch_shapes=[pltpu.VMEM((1, LANES), F32)],
        compiler_params=_params(("arbitrary", "arbitrary")),
        name="fox_proj",
    )(x, mod, g, wf, bf, tril, pq, pk, wq, wk, wv)


def _attn_kernel(q_ref, k_ref, v_ref, o_ref, *, chunk_mask):
    t = q_ref.shape[1]
    qi = pl.program_id(2)
    row = lax.broadcasted_iota(I32, (t, t), 0)
    col = lax.broadcasted_iota(I32, (t, t), 1)
    if chunk_mask:
        shift = CHUNK.bit_length() - 1
        allowed = lax.shift_right_logical(row, shift) >= lax.shift_right_logical(col, shift)
    else:
        allowed = row >= col

    outs = []
    for h in range(2):
        q = q_ref[0, :, h * HEAD_PAD:(h + 1) * HEAD_PAD]

        def scores(kj, h=h, q=q):
            start = pl.multiple_of(kj * t, t)
            k = k_ref[0, pl.ds(start, t), h * HEAD_PAD:(h + 1) * HEAD_PAD]
            v = v_ref[0, pl.ds(start, t), :]
            return _dot_nt(q, k), v

        s, v = scores(qi)
        s = jnp.where(allowed, s, -jnp.inf)
        m = jnp.max(s, axis=1, keepdims=True)
        p = jnp.exp(s - m)
        l = jnp.sum(p, axis=1, keepdims=True)
        acc = _dot(p.astype(BF16), v)

        def body(kj, carry, scores=scores):
            m, l, acc = carry
            s, v = scores(kj)
            m_new = jnp.maximum(m, jnp.max(s, axis=1, keepdims=True))
            alpha = jnp.exp(m - m_new)
            p = jnp.exp(s - m_new)
            l = alpha * l + jnp.sum(p, axis=1, keepdims=True)
            acc = alpha * acc + _dot(p.astype(BF16), v)
            return m_new, l, acc

        m, l, acc = lax.fori_loop(0, qi, body, (m, l, acc))
        outs.append(acc / l)

    lane = lax.broadcasted_iota(I32, (t, 2 * V_DIM), 1)
    o_ref[0] = jnp.where(lane < V_DIM, outs[0], outs[1]).astype(o_ref.dtype)


def _attn_call(q, k, v, chunk_mask):
    bsz, s, _ = q.shape
    t = ATTN_TILE
    return pl.pallas_call(
        functools.partial(_attn_kernel, chunk_mask=chunk_mask),
        grid=(bsz, N_HEADS // 2, s // t),
        in_specs=[
            pl.BlockSpec((1, t, 2 * HEAD_PAD), lambda b, p, i: (b, i, p)),
            pl.BlockSpec((1, s, 2 * HEAD_PAD), lambda b, p, i: (b, 0, p)),
            pl.BlockSpec((1, s, 2 * V_DIM), lambda b, p, i: (b, 0, p)),
        ],
        out_specs=pl.BlockSpec((1, t, 2 * V_DIM), lambda b, p, i: (b, i, p)),
        out_shape=jax.ShapeDtypeStruct((bsz, s, N_HEADS * V_DIM), BF16),
        compiler_params=_params(("arbitrary", "arbitrary", "arbitrary")),
        name="attention",
    )(q, k, v)


def _top2_of4(a):
    m1 = jnp.maximum(jnp.maximum(a[0], a[1]), jnp.maximum(a[2], a[3]))
    i1 = jnp.where(a[0] == m1, 0.0, jnp.where(a[1] == m1, 1.0, jnp.where(a[2] == m1, 2.0, 3.0)))
    b = [jnp.where(i1 == float(i), -jnp.inf, a[i]) for i in range(4)]
    m2 = jnp.maximum(jnp.maximum(b[0], b[1]), jnp.maximum(b[2], b[3]))
    i2 = jnp.where(b[0] == m2, 0.0, jnp.where(b[1] == m2, 1.0, jnp.where(b[2] == m2, 2.0, 3.0)))
    return m1, m2, i1, i2


def _pick(idx, vals):
    return jnp.where(idx == 0.0, vals[0], jnp.where(idx == 1.0, vals[1],
                                                   jnp.where(idx == 2.0, vals[2], vals[3])))


def _out_router_kernel(a_ref, wo_ref, x_ref, mod_ref, g_ref, rwh_ref, rwl_ref, rb_ref,
                       x1_ref, hp_ref, rout_ref):
    tm = a_ref.shape[1]
    mix = _dot(a_ref[0], wo_ref[...])
    x1 = x_ref[0] + mod_ref[0, 2:3, :] * mix
    x1_ref[0] = x1
    h = _norm_mod(x1, g_ref[...], mod_ref[0, 3:4, :], mod_ref[0, 4:5, :])
    hb = h.astype(BF16)
    hl = (h - hb.astype(F32)).astype(BF16)

    logits = _dot_nt(rwh_ref[...], hb) + _dot_nt(rwl_ref[...], hb) + _dot_nt(rwh_ref[...], hl)
    score = jax.nn.sigmoid(logits)
    sel = score + rb_ref[...]
    srow = [score[e:e + 1, :] for e in range(N_EXPERTS)]
    arow = [sel[e:e + 1, :] for e in range(N_EXPERTS)]

    gs = []
    for g in range(N_GROUPS):
        m1, m2, _, _ = _top2_of4(arow[4 * g:4 * g + 4])
        gs.append(m1 + m2)
    gmax = jnp.maximum(jnp.maximum(gs[0], gs[1]), jnp.maximum(gs[2], gs[3]))
    gid = jnp.where(gs[0] == gmax, 0.0, jnp.where(gs[1] == gmax, 1.0,
                                                jnp.where(gs[2] == gmax, 2.0, 3.0)))
    a_in = [_pick(gid, [arow[4 * g + i] for g in range(N_GROUPS)]) for i in range(4)]
    s_in = [_pick(gid, [srow[4 * g + i] for g in range(N_GROUPS)]) for i in range(4)]
    _, _, i1, i2 = _top2_of4(a_in)
    w1 = _pick(i1, s_in)
    w2 = _pick(i2, s_in)
    den = w1 + w2
    w1 = w1 / den
    w2 = w2 / den
    lo = jnp.minimum(i1, i2)
    hi = jnp.maximum(i1, i2)
    w_lo = jnp.where(i1 < i2, w1, w2)
    w_hi = jnp.where(i1 < i2, w2, w1)
    pair = jnp.where(lo == 0.0, hi - 1.0, jnp.where(lo == 1.0, hi + 1.0, 5.0))
    cls = gid * float(N_PAIRS) + pair
    zero = jnp.zeros_like(cls)
    rout_ref[...] = jnp.concatenate([cls, w_lo, w_hi, zero, zero, zero, zero, zero], axis=0)

    wl_col = jnp.transpose(jnp.broadcast_to(w_lo, (LANES, tm)))
    wh_col = jnp.transpose(jnp.broadcast_to(w_hi, (LANES, tm)))

    hp_ref[:, 0:D_MODEL] = h
    hp_ref[:, D_MODEL:D_MODEL + LANES] = wl_col
    hp_ref[:, D_MODEL + LANES:D_MODEL + 2 * LANES] = wh_col


def _out_router_call(attn, wo, x, mod, g, rwh, rwl, rb):
    bsz, s, d = x.shape
    tm = ROW_TILE
    n = bsz * s
    nt = s // tm
    row = lambda w: pl.BlockSpec((1, tm, w), lambda b, i: (b, i, 0))
    return pl.pallas_call(
        _out_router_kernel,
        grid=(bsz, nt),
        in_specs=[
            row(d),
            _const_spec(wo.shape),
            row(d),
            pl.BlockSpec((1, 6, d), lambda b, i: (b, 0, 0)),
            _const_spec((1, d)),
            _const_spec(rwh.shape), _const_spec(rwl.shape), _const_spec(rb.shape),
        ],
        out_specs=[
            row(d),
            pl.BlockSpec((tm, PACK_W), lambda b, i: (b * nt + i, 0)),
            pl.BlockSpec((8, tm), lambda b, i: (0, b * nt + i)),
        ],
        out_shape=[jax.ShapeDtypeStruct((bsz, s, d), F32),
                   jax.ShapeDtypeStruct((n, PACK_W), F32),
                   jax.ShapeDtypeStruct((8, n), F32)],
        compiler_params=_params(("arbitrary", "arbitrary")),
        name="out_router",
    )(attn, wo, x, mod, g, rwh, rwl, rb)


def _rank_kernel(cls_ref, triu_ref, rank_ref, count_ref, carry_ref):
    @pl.when(pl.program_id(0) == 0)
    def _():
        carry_ref[...] = jnp.zeros_like(carry_ref)

    c = cls_ref[...]
    tc = c.shape[1]
    sub = lax.broadcasted_iota(I32, (CLASS_PAD, tc), 0).astype(F32)
    onehot = jnp.where(sub == c, 1.0, 0.0)
    prefix = _dot(onehot.astype(BF16), triu_ref[...])
    carry = carry_ref[...]
    rank_ref[...] = jnp.sum(onehot * (prefix - 1.0 + carry), axis=0, keepdims=True)
    total = carry + jnp.broadcast_to(prefix[:, tc - 1:tc], carry.shape)
    carry_ref[...] = total
    count_ref[...] = total[:, 0:LANES]


def _rank_call(cls_row, triu):
    n = cls_row.shape[1]
    tc = RANK_TILE
    return pl.pallas_call(
        _rank_kernel,
        grid=(n // tc,),
        in_specs=[pl.BlockSpec((1, tc), lambda i: (0, i)), _const_spec(triu.shape)],
        out_specs=[pl.BlockSpec((1, tc), lambda i: (0, i)),
                   pl.BlockSpec((CLASS_PAD, LANES), lambda i: (0, 0))],
        out_shape=[jax.ShapeDtypeStruct((1, n), F32),
                   jax.ShapeDtypeStruct((CLASS_PAD, LANES), F32)],
        scratch_shapes=[pltpu.VMEM((CLASS_PAD, tc), F32)],
        compiler_params=_params(("arbitrary",)),
        name="rank",
    )(cls_row, triu)


def _moe_kernel(src_ref, elo_ref, ehi_ref, nused_ref,
                hp_hbm, wg_lo, wu_lo, wd_lo, wg_hi, wu_hi, wd_hi, y_hbm,
                xbuf, obuf, gsem, ssem, *, n_tokens):
    tm = MOE_TILE
    t = pl.program_id(0)
    n_used = nused_ref[0]
    slot = t % 2

    def gather_rows(tile, slot_):
        base = tile * tm
        for r in range(tm):
            idx = jnp.maximum(src_ref[base + r], 0)
            pltpu.make_async_copy(hp_hbm.at[pl.ds(idx, 1), :],
                                  xbuf.at[slot_, pl.ds(r, 1), :],
                                  gsem.at[slot_]).start()

    def wait_gather(slot_):
        pltpu.make_async_copy(hp_hbm.at[pl.ds(0, tm), :], xbuf.at[slot_], gsem.at[slot_]).wait()

    def wait_scatter(slot_):
        pltpu.make_async_copy(obuf.at[slot_], y_hbm.at[pl.ds(0, tm), :], ssem.at[slot_]).wait()

    @pl.when(t == 0)
    def _():
        obuf[...] = jnp.zeros_like(obuf)
        for s_ in range(2):
            fill = pltpu.make_async_copy(obuf.at[s_], y_hbm.at[pl.ds(n_tokens + s_ * tm, tm), :],
                                         ssem.at[s_])
            fill.start()
            fill.wait()

    @pl.when(jnp.logical_and(t == 0, n_used > 0))
    def _():
        gather_rows(0, 0)

    @pl.when(t + 1 < n_used)
    def _():
        gather_rows(t + 1, 1 - slot)

    @pl.when(t < n_used)
    def _():
        wait_gather(slot)

        @pl.when(t >= 2)
        def _():
            wait_scatter(slot)

        xb = xbuf[slot, :, 0:D_MODEL].astype(BF16)
        w_lo = xbuf[slot, :, D_MODEL:D_MODEL + LANES]
        w_hi = xbuf[slot, :, D_MODEL + LANES:D_MODEL + 2 * LANES]

        def expert(wg, wu, wd):
            gate = _dot(xb, wg[0])
            up = _dot(xb, wu[0])
            act = gate * jax.nn.sigmoid(gate) * up
            return _dot(act.astype(BF16), wd[0])

        reps = D_MODEL // LANES
        out = (jnp.tile(w_lo, (1, reps)) * expert(wg_lo, wu_lo, wd_lo)
               + jnp.tile(w_hi, (1, reps)) * expert(wg_hi, wu_hi, wd_hi))
        obuf[slot] = out

        base = t * tm
        for r in range(tm):
            idx = src_ref[base + r]
            dst = jnp.where(idx < 0, n_tokens + slot * tm + r, idx)
            pltpu.make_async_copy(obuf.at[slot, pl.ds(r, 1), :],
                                  y_hbm.at[pl.ds(dst, 1), :],
                                  ssem.at[slot]).start()

        @pl.when(t == n_used - 1)
        def _():
            wait_scatter(slot)

            @pl.when(t >= 1)
            def _():
                wait_scatter(1 - slot)


def _moe_call(src, elo, ehi, n_used, hp, wg, wu, wd, n_tokens):
    tm = MOE_TILE
    n_tiles = src.shape[0] // tm
    de = D_EXPERT
    d = D_MODEL
    lo3 = lambda t, s, el, eh, nu: (el[t], 0, 0)
    hi3 = lambda t, s, el, eh, nu: (eh[t], 0, 0)
    grid_spec = pltpu.PrefetchScalarGridSpec(
        num_scalar_prefetch=4,
        grid=(n_tiles,),
        in_specs=[
            pl.BlockSpec(memory_space=pl.ANY),
            pl.BlockSpec((1, d, de), lo3), pl.BlockSpec((1, d, de), lo3), pl.BlockSpec((1, de, d), lo3),
            pl.BlockSpec((1, d, de), hi3), pl.BlockSpec((1, d, de), hi3), pl.BlockSpec((1, de, d), hi3),
        ],
        out_specs=pl.BlockSpec(memory_space=pl.ANY),
        scratch_shapes=[
            pltpu.VMEM((2, tm, PACK_W), F32),
            pltpu.VMEM((2, tm, d), F32),
            pltpu.SemaphoreType.DMA((2,)),
            pltpu.SemaphoreType.DMA((2,)),
        ],
    )
    return pl.pallas_call(
        functools.partial(_moe_kernel, n_tokens=n_tokens),
        grid_spec=grid_spec,
        out_shape=jax.ShapeDtypeStruct((n_tokens + 2 * tm, d), F32),
        compiler_params=_params(("arbitrary",)),
        name="moe",
    )(src, elo, ehi, n_used, hp, wg, wu, wd, wg, wu, wd)


def _residual_kernel(x_ref, y_ref, mod_ref, o_ref):
    o_ref[0] = x_ref[0] + mod_ref[0, 5:6, :] * y_ref[...]


def _residual_norm_kernel(x_ref, y_ref, mod_ref, g_ref, o_ref):
    x = x_ref[0] + mod_ref[0, 5:6, :] * y_ref[...]
    o_ref[0] = x * lax.rsqrt(jnp.mean(x * x, axis=-1, keepdims=True) + EPS) * g_ref[...]


def _residual_call(x, y, mod, final_g=None):
    bsz, s, d = x.shape
    tm = ROW_TILE
    nt = s // tm
    in_specs = [
        pl.BlockSpec((1, tm, d), lambda b, i: (b, i, 0)),
        pl.BlockSpec((tm, d), lambda b, i: (b * nt + i, 0)),
        pl.BlockSpec((1, 6, d), lambda b, i: (b, 0, 0)),
    ]
    args = [x, y, mod]
    kern = _residual_kernel
    if final_g is not None:
        in_specs.append(_const_spec((1, d)))
        args.append(final_g)
        kern = _residual_norm_kernel
    return pl.pallas_call(
        kern,
        grid=(bsz, nt),
        in_specs=in_specs,
        out_specs=pl.BlockSpec((1, tm, d), lambda b, i: (b, i, 0)),
        out_shape=jax.ShapeDtypeStruct((bsz, s, d), F32),
        compiler_params=_params(("arbitrary", "arbitrary")),
        name="residual",
    )(*args)


def _kv_proj_kernel(x_ref, mod_ref, g_ref, wd_ref, lg_ref, wk_ref, wv_ref, cs_ref, sn_ref,
                    k_ref, v_ref):
    h = _norm_mod(x_ref[0], g_ref[...], mod_ref[0, 0:1, :], mod_ref[0, 1:2, :])
    dn = _dot(h.astype(BF16), wd_ref[...])
    ckv = dn[:, 0:KV_RANK]
    c = ckv * lax.rsqrt(jnp.mean(ckv * ckv, axis=-1, keepdims=True) + EPS) * lg_ref[...]
    cb = c.astype(BF16)
    kr = (dn[:, KV_RANK:KV_RANK + LANES] * cs_ref[0]
          + dn[:, KV_RANK + LANES:KV_RANK + 2 * LANES] * sn_ref[0])
    k_ref[0] = (_dot(cb, wk_ref[...]) + jnp.tile(kr, (1, N_HEADS))).astype(BF16)
    v_ref[0] = _dot(cb, wv_ref[...]).astype(BF16)


def _kv_proj_call(x, mod, g, wd, lg, wk, wv, cs, sn):
    bsz, s, d = x.shape
    tm = ROW_TILE
    hp = N_HEADS * HEAD_PAD
    row = lambda w: pl.BlockSpec((1, tm, w), lambda b, i: (b, i, 0))
    return pl.pallas_call(
        _kv_proj_kernel,
        grid=(bsz, s // tm),
        in_specs=[
            row(d),
            pl.BlockSpec((1, 2, d), lambda b, i: (b, 0, 0)),
            _const_spec((1, d)),
            _const_spec(wd.shape), _const_spec(lg.shape), _const_spec(wk.shape), _const_spec(wv.shape),
            row(LANES), row(LANES),
        ],
        out_specs=[row(hp), row(d)],
        out_shape=[jax.ShapeDtypeStruct((bsz, s, hp), BF16),
                   jax.ShapeDtypeStruct((bsz, s, d), BF16)],
        compiler_params=_params(("arbitrary", "arbitrary")),
        name="kv_proj",
    )(x, mod, g, wd, lg, wk, wv, cs, sn)


def _q_proj_kernel(x_ref, mod_ref, g_ref, wdq_ref, qg_ref, wuq_ref, tab_ref, q_ref):
    h = _norm_mod(x_ref[0], g_ref[...], mod_ref[0, 0:1, :], mod_ref[0, 1:2, :])
    cq = _dot(h.astype(BF16), wdq_ref[...])
    cq = cq * lax.rsqrt(jnp.mean(cq * cq, axis=-1, keepdims=True) + EPS) * qg_ref[...]
    q = _dot(cq.astype(BF16), wuq_ref[...])
    q_ref[0] = (q * jnp.tile(tab_ref[0], (1, N_HEADS))).astype(BF16)


def _q_proj_call(x, mod, g, wdq, qg, wuq, tab):
    bsz, s, d = x.shape
    tm = ROW_TILE
    hp = N_HEADS * HEAD_PAD
    row = lambda w: pl.BlockSpec((1, tm, w), lambda b, i: (b, i, 0))
    return pl.pallas_call(
        _q_proj_kernel,
        grid=(bsz, s // tm),
        in_specs=[
            row(d),
            pl.BlockSpec((1, 6, d), lambda b, i: (b, 0, 0)),
            _const_spec((1, d)),
            _const_spec(wdq.shape), _const_spec(qg.shape), _const_spec(wuq.shape),
            row(LANES),
        ],
        out_specs=row(hp),
        out_shape=jax.ShapeDtypeStruct((bsz, s, hp), BF16),
        compiler_params=_params(("arbitrary", "arbitrary")),
        name="q_proj",
    )(x, mod, g, wdq, qg, wuq, tab)


def _pad_heads(w, width):
    k = w.shape[0]
    w = w.reshape(k, N_HEADS, width)
    w = jnp.pad(w, ((0, 0), (0, 0), (0, HEAD_PAD - width)))
    return w.reshape(k, N_HEADS * HEAD_PAD)


def _rot_cols(w):
    half = ROPE_DIM // 2
    return jnp.concatenate([-w[..., half:], w[..., :half]], axis=-1)


def _bias_placement():
    rows = jnp.arange(LANES)[:, None]
    cols = jnp.arange(N_HEADS * HEAD_PAD)[None, :]
    head = cols // HEAD_PAD
    off = cols % HEAD_PAD - HEAD_DIM
    part = jnp.where((off >= 0) & (off < 3), off, -1)
    is_val = (part >= 0) & (rows == part * N_HEADS + head)
    is_one = (off >= 3) & (off < 6) & (rows == 3 * N_HEADS)
    pq = jnp.where(is_val | is_one, 1.0, 0.0)
    part_k = jnp.where((off >= 3) & (off < 6), off - 3, -1)
    is_val_k = (part_k >= 0) & (rows == part_k * N_HEADS + head)
    is_one_k = (off >= 0) & (off < 3) & (rows == 3 * N_HEADS)
    pk = jnp.where(is_val_k, -1.0, 0.0) + jnp.where(is_one_k, 1.0, 0.0)
    return pq.astype(BF16), pk.astype(BF16)


def _moe_plan(rout, rank, counts, n_tokens):
    tm = MOE_TILE
    n_tiles = n_tokens // tm + N_CLASSES
    cls = rout[0].astype(I32)
    cnt = counts[:N_CLASSES, 0].astype(I32)
    tiles = (cnt + tm - 1) // tm
    tile_end = jnp.cumsum(tiles)
    tile_start = tile_end - tiles
    pos = tile_start[cls] * tm + rank[0].astype(I32)
    src = jnp.full((n_tiles * tm,), -1, I32).at[pos].set(jnp.arange(n_tokens, dtype=I32))
    n_used = tile_end[-1]
    tile_id = jnp.minimum(jnp.arange(n_tiles, dtype=I32), n_used - 1)
    tile_cls = jnp.sum((tile_end[None, :] <= tile_id[:, None]).astype(I32), axis=1)
    grp = tile_cls // N_PAIRS
    pair = tile_cls % N_PAIRS
    lo_of = jnp.array([0, 0, 0, 1, 1, 2], I32)
    hi_of = jnp.array([1, 2, 3, 2, 3, 3], I32)
    elo = grp * EXPERTS_PER_GROUP + lo_of[pair]
    ehi = grp * EXPERTS_PER_GROUP + hi_of[pair]
    return src, elo, ehi, n_used.reshape(1).astype(I32)


def _moe_layer(x1, hp, rout, mod, wg, wu, wd, triu, final_g=None):
    bsz, s, _ = x1.shape
    n = bsz * s
    rank, counts = _rank_call(rout[0:1], triu)
    src, elo, ehi, n_used = _moe_plan(rout, rank, counts, n)
    y = _moe_call(src, elo, ehi, n_used, hp, wg, wu, wd, n)
    return _residual_call(x1, y, mod, final_g)


def kernel(x, c, positions, a_norm_g, a_w_in, a_b_f, a_w_o, kv_norm_g, kv_w_mod, kv_b_mod, kv_w_down, kv_latent_g, kv_w_up, b_norm_g, b_w_dq, b_q_norm_g, b_w_uq, b_w_o, w_mod, b_mod, ffn_norm_g, router_w, router_bias, exp_w_gate, exp_w_up, exp_w_down, final_norm_g):
    bsz, s, d = x.shape
    n = bsz * s

    mod0 = _mod_call(c, w_mod[0], b_mod[0]).reshape(bsz, 6, d)
    mod1 = _mod_call(c, w_mod[1], b_mod[1]).reshape(bsz, 6, d)
    modkv = _mod_call(c, kv_w_mod, kv_b_mod).reshape(bsz, 2, d)

    w_in = a_w_in[0]
    wq = _pad_heads(w_in[:, 0:d], HEAD_DIM).astype(BF16)
    wk = _pad_heads(w_in[:, d:2 * d], HEAD_DIM).astype(BF16)
    wv = w_in[:, 2 * d:3 * d].astype(BF16)
    wf16 = w_in[:, 3 * d:]
    wf = jnp.pad(jnp.concatenate([wf16, wf16, wf16], axis=1), ((0, 0), (0, LANES - 3 * N_HEADS))).astype(BF16)
    bf = jnp.pad(jnp.concatenate([a_b_f[0]] * 3), (0, LANES - 3 * N_HEADS)).reshape(1, LANES)
    tril = jnp.tril(jnp.ones((ROW_TILE, ROW_TILE), F32)).astype(BF16)
    triu = jnp.triu(jnp.ones((RANK_TILE, RANK_TILE), F32)).astype(BF16)
    pq, pk = _bias_placement()

    rw_t = router_w.T
    rwh = rw_t.astype(BF16)
    rwl = (rw_t - rwh.astype(F32)).astype(BF16)
    rb = jnp.broadcast_to(router_bias.astype(F32)[:, None], (N_EXPERTS, ROW_TILE))

    ewg = exp_w_gate.astype(BF16)
    ewu = exp_w_up.astype(BF16)
    ewd = exp_w_down.astype(BF16)

    half = ROPE_DIM // 2
    inv_freq = ROPE_THETA ** (-jnp.arange(half, dtype=F32) / half)
    ang = positions.astype(F32)[..., None] * inv_freq
    cos2 = jnp.concatenate([jnp.cos(ang)] * 2, axis=-1)
    sin2 = jnp.concatenate([jnp.sin(ang)] * 2, axis=-1)
    zeros64 = jnp.zeros((bsz, s, NOPE_DIM), F32)
    qscale = (NOPE_DIM + ROPE_DIM) ** -0.5
    q_tab = jnp.concatenate([jnp.ones_like(zeros64), cos2, sin2], axis=-1) * qscale
    k_cos = jnp.concatenate([zeros64, cos2, cos2], axis=-1)
    k_sin = jnp.concatenate([zeros64, sin2, sin2], axis=-1)

    w_ckv = kv_w_down[:, :KV_RANK]
    w_kr = kv_w_down[:, KV_RANK:]
    zcol = jnp.zeros((d, NOPE_DIM), F32)
    w_down = jnp.concatenate([w_ckv, zcol, w_kr, w_kr, zcol, _rot_cols(w_kr), _rot_cols(w_kr)], axis=1).astype(BF16)
    w_up3 = kv_w_up.reshape(KV_RANK, N_HEADS, NOPE_DIM + V_DIM)
    w_upk = _pad_heads(w_up3[:, :, :NOPE_DIM].reshape(KV_RANK, N_HEADS * NOPE_DIM), NOPE_DIM).astype(BF16)
    w_upv = w_up3[:, :, NOPE_DIM:].reshape(KV_RANK, N_HEADS * V_DIM).astype(BF16)

    w_uq3 = b_w_uq[0].reshape(Q_RANK, N_HEADS, NOPE_DIM + ROPE_DIM)
    w_uq = jnp.concatenate([w_uq3, _rot_cols(w_uq3[:, :, NOPE_DIM:])], axis=-1)
    w_uq = w_uq.reshape(Q_RANK, N_HEADS * HEAD_PAD).astype(BF16)

    qa, ka, va = _fox_proj_call(x, mod0, a_norm_g[0].reshape(1, d), wf, bf, tril, pq, pk, wq, wk, wv)
    attn = _attn_call(qa, ka, va, chunk_mask=False)
    x1, hp, rout = _out_router_call(attn, a_w_o[0].astype(BF16), x, mod0, ffn_norm_g[0].reshape(1, d),
                                    rwh, rwl, rb)
    x2 = _moe_layer(x1, hp, rout, mod0, ewg[0], ewu[0], ewd[0], triu)

    kb, vb = _kv_proj_call(x2, modkv, kv_norm_g.reshape(1, d), w_down, kv_latent_g.reshape(1, KV_RANK),
                           w_upk, w_upv, k_cos, k_sin)

    qb = _q_proj_call(x2, mod1, b_norm_g[0].reshape(1, d), b_w_dq[0].astype(BF16),
                      b_q_norm_g[0].reshape(1, Q_RANK), w_uq, q_tab)
    attn = _attn_call(qb, kb, vb, chunk_mask=True)
    x3, hp, rout = _out_router_call(attn, b_w_o[0].astype(BF16), x2, mod1, ffn_norm_g[1].reshape(1, d),
                                    rwh, rwl, rb)
    return _moe_layer(x3, hp, rout, mod1, ewg[1], ewu[1], ewd[1], triu, final_norm_g.reshape(1, d))
```

```python
import functools

import jax
import jax.numpy as jnp
from jax import lax
from jax.experimental import pallas as pl
from jax.experimental.pallas import tpu as pltpu

F32 = jnp.float32
BF16 = jnp.bfloat16
I32 = jnp.int32
U32 = jnp.uint32

D_MODEL = 1024
N_HEADS = 16
HEAD_DIM = 64
NOPE_DIM = 64
ROPE_DIM = 32
V_DIM = 64
Q_RANK = 768
KV_RANK = 256
ROPE_THETA = 10000.0
CHUNK = 64
N_EXPERTS = 16
N_GROUPS = 4
EXPERTS_PER_GROUP = 4
D_EXPERT = 512
EPS = 1e-6

LANES = 128
HEAD_PAD = LANES
N_PAIRS = 6
N_CLASSES = N_GROUPS * N_PAIRS
CLASS_PAD = 32
ROW_TILE = 256
ATTN_Q_TILE = 512
MOE_TILE = 256
RANK_TILE = 512
PACK_W = D_MODEL + 2 * LANES
VMEM_LIMIT = 56 * 1024 * 1024


def _params(sem):
    return pltpu.CompilerParams(dimension_semantics=sem, vmem_limit_bytes=VMEM_LIMIT)


def _const_spec(shape):
    zeros = (0,) * len(shape)
    return pl.BlockSpec(shape, lambda *_: zeros)


def _norm_mod(x, g, shift, scale):
    y = x * lax.rsqrt(jnp.mean(x * x, axis=-1, keepdims=True) + EPS) * g
    return y * (1.0 + scale) + shift


def _split3(x):
    a = x.astype(BF16).astype(F32)
    r = x - a
    b = r.astype(BF16).astype(F32)
    c = (r - b).astype(BF16).astype(F32)
    return a, b, c


def _dot(a, b):
    return jnp.dot(a, b, preferred_element_type=F32)


def _dot_nt(a, b):
    return lax.dot_general(a, b, (((1,), (1,)), ((), ())), preferred_element_type=F32)


def _mod_kernel(c_ref, w_ref, b_ref, o_ref):
    c = c_ref[...]
    ca = c * jax.nn.sigmoid(c)
    o_ref[...] = _dot(ca.astype(BF16), w_ref[...].astype(BF16)) + b_ref[...]


def _mod_call(c, w, b):
    bsz, d = c.shape
    n = w.shape[1]
    tn = 1024
    return pl.pallas_call(
        _mod_kernel,
        grid=(n // tn,),
        in_specs=[
            pl.BlockSpec((bsz, d), lambda j: (0, 0)),
            pl.BlockSpec((d, tn), lambda j: (0, j)),
            pl.BlockSpec((1, tn), lambda j: (0, j)),
        ],
        out_specs=pl.BlockSpec((bsz, tn), lambda j: (0, j)),
        out_shape=jax.ShapeDtypeStruct((bsz, n), F32),
        compiler_params=_params(("arbitrary",)),
        name="mod",
    )(c, w, b.reshape(1, n))


def _fox_proj_kernel(x_ref, mod_ref, g_ref, wf_ref, bf_ref, tril_ref, pq_ref, pk_ref,
                     wq_ref, wk_ref, wv_ref, q_ref, k_ref, v_ref, carry_ref):
    @pl.when(pl.program_id(1) == 0)
    def _():
        carry_ref[...] = jnp.zeros_like(carry_ref)

    h = _norm_mod(x_ref[0], g_ref[...], mod_ref[0, 0:1, :], mod_ref[0, 1:2, :])
    hb = h.astype(BF16)

    z = _dot(hb, wf_ref[...]) + bf_ref[...]
    logf = jnp.minimum(z, 0.0) - jnp.log(1.0 + jnp.exp(-jnp.abs(z)))
    tril = tril_ref[...]
    a, b, c = _split3(logf)
    cum = carry_ref[...] + (_dot(tril, a.astype(BF16)) + _dot(tril, b.astype(BF16))
                            + _dot(tril, c.astype(BF16)))
    carry_ref[...] = cum[-1:, :]

    hi, mid, lo = _split3(cum)
    lane = lax.broadcasted_iota(I32, cum.shape, 1)
    e = jnp.where(lane < N_HEADS, hi,
                  jnp.where(lane < 2 * N_HEADS, mid,
                            jnp.where(lane < 3 * N_HEADS, lo,
                                      jnp.where(lane == 3 * N_HEADS, 1.0, 0.0)))).astype(BF16)
    scale = HEAD_DIM ** -0.5
    q_ref[0] = (_dot(hb, wq_ref[...]) * scale + _dot(e, pq_ref[...])).astype(BF16)
    k_ref[0] = (_dot(hb, wk_ref[...]) + _dot(e, pk_ref[...])).astype(BF16)
    v_ref[0, 0] = _dot_nt(wv_ref[...], hb).astype(BF16)


def _fox_proj_call(x, mod, g, wf, bf, tril, pq, pk, wq, wk, wv):
    bsz, s, d = x.shape
    tm = ROW_TILE
    hp = N_HEADS * HEAD_PAD
    row = lambda w: pl.BlockSpec((1, tm, w), lambda b, i: (b, i, 0))
    return pl.pallas_call(
        _fox_proj_kernel,
        grid=(bsz, s // tm),
        in_specs=[
            row(d),
            pl.BlockSpec((1, 6, d), lambda b, i: (b, 0, 0)),
            _const_spec((1, d)),
            _const_spec(wf.shape), _const_spec(bf.shape), _const_spec(tril.shape),
            _const_spec(pq.shape), _const_spec(pk.shape),
            _const_spec(wq.shape), _const_spec(wk.shape), _const_spec(wv.shape),
        ],
        out_specs=[row(hp), row(hp), pl.BlockSpec((1, 1, d, tm), lambda b, i: (b, i, 0, 0))],
        out_shape=[jax.ShapeDtypeStruct((bsz, s, hp), BF16),
                   jax.ShapeDtypeStruct((bsz, s, hp), BF16),
                   jax.ShapeDtypeStruct((bsz, s // tm, d, tm), BF16)],
        scratch_shapes=[pltpu.VMEM((1, LANES), F32)],
        compiler_params=_params(("arbitrary", "arbitrary")),
        name="fox_proj",
    )(x, mod, g, wf, bf, tril, pq, pk, wq, wk, wv)


def _attn_kernel(q_ref, k_ref, vt_ref, o_ref, *, chunk_mask):
    tq = q_ref.shape[1]
    tk = vt_ref.shape[3]
    r = tq // tk
    qi = pl.program_id(2)
    key = lax.broadcasted_iota(I32, (tk, tq), 0)
    qry = lax.broadcasted_iota(I32, (tk, tq), 1)
    shift = CHUNK.bit_length() - 1

    qs = [q_ref[0, :, h * HEAD_PAD:(h + 1) * HEAD_PAD] for h in range(2)]

    def group(base, carry, masked):
        sts = []
        for d in range(r):
            start = pl.multiple_of((base + d) * tk, tk)
            sts.append([_dot_nt(k_ref[0, pl.ds(start, tk), h * HEAD_PAD:(h + 1) * HEAD_PAD], qs[h])
                        for h in range(2)])
        carry = list(carry)
        for d in range(r):
            stats = []
            for h in range(2):
                st = sts[d][h]
                if masked:
                    kpos = key + d * tk
                    if chunk_mask:
                        allowed = lax.shift_right_logical(qry, shift) >= lax.shift_right_logical(kpos, shift)
                    else:
                        allowed = qry >= kpos
                    st = jnp.where(allowed, st, -jnp.inf)
                m, l = carry[3 * h], carry[3 * h + 1]
                m_new = jnp.maximum(m, jnp.max(st, axis=0, keepdims=True))
                alpha = jnp.exp(m - m_new)
                p = jnp.exp(st - m_new)
                stats.append((m_new, alpha, alpha * l + jnp.sum(p, axis=0, keepdims=True), p.astype(BF16)))
            for h in range(2):
                m_new, alpha, l_new, p = stats[h]
                vt = vt_ref[0, base + d, h * V_DIM:(h + 1) * V_DIM, :]
                carry[3 * h:3 * h + 3] = [m_new, l_new, alpha * carry[3 * h + 2] + _dot(vt, p)]
        return tuple(carry)

    init = []
    for h in range(2):
        init += [jnp.full((1, tq), -jnp.inf, F32), jnp.zeros((1, tq), F32), jnp.zeros((V_DIM, tq), F32)]
    carry = group(r * qi, tuple(init), masked=True)
    carry = lax.fori_loop(0, qi, lambda j, c: group(r * j, c, masked=False), carry)
    ot = jnp.concatenate([carry[2] / carry[1], carry[5] / carry[4]], axis=0)
    o_ref[0] = jnp.transpose(ot).astype(o_ref.dtype)


def _attn_call(q, k, vt, chunk_mask):
    bsz, s, _ = q.shape
    tq = ATTN_Q_TILE
    nk, tk = vt.shape[1], vt.shape[3]
    return pl.pallas_call(
        functools.partial(_attn_kernel, chunk_mask=chunk_mask),
        grid=(bsz, N_HEADS // 2, s // tq),
        in_specs=[
            pl.BlockSpec((1, tq, 2 * HEAD_PAD), lambda b, p, i: (b, i, p)),
            pl.BlockSpec((1, s, 2 * HEAD_PAD), lambda b, p, i: (b, 0, p)),
            pl.BlockSpec((1, nk, 2 * V_DIM, tk), lambda b, p, i: (b, 0, p, 0)),
        ],
        out_specs=pl.BlockSpec((1, tq, 2 * V_DIM), lambda b, p, i: (b, i, p)),
        out_shape=jax.ShapeDtypeStruct((bsz, s, N_HEADS * V_DIM), BF16),
        compiler_params=_params(("arbitrary", "arbitrary", "arbitrary")),
        name="attention",
    )(q, k, vt)


def _top2_of4(a):
    m1 = jnp.maximum(jnp.maximum(a[0], a[1]), jnp.maximum(a[2], a[3]))
    i1 = jnp.where(a[0] == m1, 0.0, jnp.where(a[1] == m1, 1.0, jnp.where(a[2] == m1, 2.0, 3.0)))
    b = [jnp.where(i1 == float(i), -jnp.inf, a[i]) for i in range(4)]
    m2 = jnp.maximum(jnp.maximum(b[0], b[1]), jnp.maximum(b[2], b[3]))
    i2 = jnp.where(b[0] == m2, 0.0, jnp.where(b[1] == m2, 1.0, jnp.where(b[2] == m2, 2.0, 3.0)))
    return m1, m2, i1, i2


def _pick(idx, vals):
    return jnp.where(idx == 0.0, vals[0], jnp.where(idx == 1.0, vals[1],
                                                   jnp.where(idx == 2.0, vals[2], vals[3])))


def _out_router_kernel(a_ref, wo_ref, x_ref, mod_ref, g_ref, rwh_ref, rwl_ref, rb_ref,
                       x1_ref, hp_ref, rout_ref):
    tm = a_ref.shape[1]
    mix = _dot(a_ref[0], wo_ref[...])
    x1 = x_ref[0] + mod_ref[0, 2:3, :] * mix
    x1_ref[0] = x1
    h = _norm_mod(x1, g_ref[...], mod_ref[0, 3:4, :], mod_ref[0, 4:5, :])
    hb = h.astype(BF16)
    hl = (h - hb.astype(F32)).astype(BF16)

    logits = _dot_nt(rwh_ref[...], hb) + _dot_nt(rwl_ref[...], hb) + _dot_nt(rwh_ref[...], hl)
    score = jax.nn.sigmoid(logits)
    sel = score + rb_ref[...]
    srow = [score[e:e + 1, :] for e in range(N_EXPERTS)]
    arow = [sel[e:e + 1, :] for e in range(N_EXPERTS)]

    gs = []
    for g in range(N_GROUPS):
        m1, m2, _, _ = _top2_of4(arow[4 * g:4 * g + 4])
        gs.append(m1 + m2)
    gmax = jnp.maximum(jnp.maximum(gs[0], gs[1]), jnp.maximum(gs[2], gs[3]))
    gid = jnp.where(gs[0] == gmax, 0.0, jnp.where(gs[1] == gmax, 1.0,
                                                jnp.where(gs[2] == gmax, 2.0, 3.0)))
    a_in = [_pick(gid, [arow[4 * g + i] for g in range(N_GROUPS)]) for i in range(4)]
    s_in = [_pick(gid, [srow[4 * g + i] for g in range(N_GROUPS)]) for i in range(4)]
    _, _, i1, i2 = _top2_of4(a_in)
    w1 = _pick(i1, s_in)
    w2 = _pick(i2, s_in)
    den = w1 + w2
    w1 = w1 / den
    w2 = w2 / den
    lo = jnp.minimum(i1, i2)
    hi = jnp.maximum(i1, i2)
    w_lo = jnp.where(i1 < i2, w1, w2)
    w_hi = jnp.where(i1 < i2, w2, w1)
    pair = jnp.where(lo == 0.0, hi - 1.0, jnp.where(lo == 1.0, hi + 1.0, 5.0))
    cls = gid * float(N_PAIRS) + pair
    zero = jnp.zeros_like(cls)
    rout_ref[...] = jnp.concatenate([cls, w_lo, w_hi, zero, zero, zero, zero, zero], axis=0)

    wl_col = jnp.transpose(jnp.broadcast_to(w_lo, (LANES, tm)))
    wh_col = jnp.transpose(jnp.broadcast_to(w_hi, (LANES, tm)))

    hp_ref[:, 0:D_MODEL] = h
    hp_ref[:, D_MODEL:D_MODEL + LANES] = wl_col
    hp_ref[:, D_MODEL + LANES:D_MODEL + 2 * LANES] = wh_col


def _out_router_call(attn, wo, x, mod, g, rwh, rwl, rb):
    bsz, s, d = x.shape
    tm = ROW_TILE
    n = bsz * s
    nt = s // tm
    row = lambda w: pl.BlockSpec((1, tm, w), lambda b, i: (b, i, 0))
    return pl.pallas_call(
        _out_router_kernel,
        grid=(bsz, nt),
        in_specs=[
            row(d),
            _const_spec(wo.shape),
            row(d),
            pl.BlockSpec((1, 6, d), lambda b, i: (b, 0, 0)),
            _const_spec((1, d)),
            _const_spec(rwh.shape), _const_spec(rwl.shape), _const_spec(rb.shape),
        ],
        out_specs=[
            row(d),
            pl.BlockSpec((tm, PACK_W), lambda b, i: (b * nt + i, 0)),
            pl.BlockSpec((8, tm), lambda b, i: (0, b * nt + i)),
        ],
        out_shape=[jax.ShapeDtypeStruct((bsz, s, d), F32),
                   jax.ShapeDtypeStruct((n, PACK_W), F32),
                   jax.ShapeDtypeStruct((8, n), F32)],
        compiler_params=_params(("arbitrary", "arbitrary")),
        name="out_router",
    )(attn, wo, x, mod, g, rwh, rwl, rb)


def _rank_kernel(cls_ref, triu_ref, rank_ref, count_ref, carry_ref):
    @pl.when(pl.program_id(0) == 0)
    def _():
        carry_ref[...] = jnp.zeros_like(carry_ref)

    c = cls_ref[...]
    tc = c.shape[1]
    sub = lax.broadcasted_iota(I32, (CLASS_PAD, tc), 0).astype(F32)
    onehot = jnp.where(sub == c, 1.0, 0.0)
    prefix = _dot(onehot.astype(BF16), triu_ref[...])
    carry = carry_ref[...]
    rank_ref[...] = jnp.sum(onehot * (prefix - 1.0 + carry), axis=0, keepdims=True)
    total = carry + jnp.broadcast_to(prefix[:, tc - 1:tc], carry.shape)
    carry_ref[...] = total
    count_ref[...] = total[:, 0:LANES]


def _rank_call(cls_row, triu):
    n = cls_row.shape[1]
    tc = RANK_TILE
    return pl.pallas_call(
        _rank_kernel,
        grid=(n // tc,),
        in_specs=[pl.BlockSpec((1, tc), lambda i: (0, i)), _const_spec(triu.shape)],
        out_specs=[pl.BlockSpec((1, tc), lambda i: (0, i)),
                   pl.BlockSpec((CLASS_PAD, LANES), lambda i: (0, 0))],
        out_shape=[jax.ShapeDtypeStruct((1, n), F32),
                   jax.ShapeDtypeStruct((CLASS_PAD, LANES), F32)],
        scratch_shapes=[pltpu.VMEM((CLASS_PAD, tc), F32)],
        compiler_params=_params(("arbitrary",)),
        name="rank",
    )(cls_row, triu)


def _moe_kernel(src_ref, elo_ref, ehi_ref, nused_ref,
                hp_hbm, wg_lo, wu_lo, wd_lo, wg_hi, wu_hi, wd_hi, y_hbm,
                xbuf, obuf, gsem, ssem, *, n_tokens):
    tm = MOE_TILE
    t = pl.program_id(0)
    n_used = nused_ref[0]
    slot = t % 2

    def gather_rows(tile, slot_):
        base = tile * tm
        for r in range(tm):
            idx = jnp.maximum(src_ref[base + r], 0)
            pltpu.make_async_copy(hp_hbm.at[pl.ds(idx, 1), :],
                                  xbuf.at[slot_, pl.ds(r, 1), :],
                                  gsem.at[slot_]).start()

    def wait_gather(slot_):
        pltpu.make_async_copy(hp_hbm.at[pl.ds(0, tm), :], xbuf.at[slot_], gsem.at[slot_]).wait()

    def wait_scatter(slot_):
        pltpu.make_async_copy(obuf.at[slot_], y_hbm.at[pl.ds(0, tm), :], ssem.at[slot_]).wait()

    @pl.when(t == 0)
    def _():
        obuf[...] = jnp.zeros_like(obuf)
        for s_ in range(2):
            fill = pltpu.make_async_copy(obuf.at[s_], y_hbm.at[pl.ds(n_tokens + s_ * tm, tm), :],
                                         ssem.at[s_])
            fill.start()
            fill.wait()

    @pl.when(jnp.logical_and(t == 0, n_used > 0))
    def _():
        gather_rows(0, 0)

    @pl.when(t + 1 < n_used)
    def _():
        gather_rows(t + 1, 1 - slot)

    @pl.when(t < n_used)
    def _():
        wait_gather(slot)

        @pl.when(t >= 2)
        def _():
            wait_scatter(slot)

        xb = xbuf[slot, :, 0:D_MODEL].astype(BF16)
        w_lo = xbuf[slot, :, D_MODEL:D_MODEL + LANES]
        w_hi = xbuf[slot, :, D_MODEL + LANES:D_MODEL + 2 * LANES]

        def expert(wg, wu, wd):
            gate = _dot(xb, wg[0])
            up = _dot(xb, wu[0])
            act = gate * jax.nn.sigmoid(gate) * up
            return _dot(act.astype(BF16), wd[0])

        reps = D_MODEL // LANES
        out = (jnp.tile(w_lo, (1, reps)) * expert(wg_lo, wu_lo, wd_lo)
               + jnp.tile(w_hi, (1, reps)) * expert(wg_hi, wu_hi, wd_hi))
        obuf[slot] = out

        base = t * tm
        for r in range(tm):
            idx = src_ref[base + r]
            dst = jnp.where(idx < 0, n_tokens + slot * tm + r, idx)
            pltpu.make_async_copy(obuf.at[slot, pl.ds(r, 1), :],
                                  y_hbm.at[pl.ds(dst, 1), :],
                                  ssem.at[slot]).start()

        @pl.when(t == n_used - 1)
        def _():
            wait_scatter(slot)

            @pl.when(t >= 1)
            def _():
                wait_scatter(1 - slot)


def _moe_call(src, elo, ehi, n_used, hp, wg, wu, wd, n_tokens):
    tm = MOE_TILE
    n_tiles = src.shape[0] // tm
    de = D_EXPERT
    d = D_MODEL
    lo3 = lambda t, s, el, eh, nu: (el[t], 0, 0)
    hi3 = lambda t, s, el, eh, nu: (eh[t], 0, 0)
    grid_spec = pltpu.PrefetchScalarGridSpec(
        num_scalar_prefetch=4,
        grid=(n_tiles,),
        in_specs=[
            pl.BlockSpec(memory_space=pl.ANY),
            pl.BlockSpec((1, d, de), lo3), pl.BlockSpec((1, d, de), lo3), pl.BlockSpec((1, de, d), lo3),
            pl.BlockSpec((1, d, de), hi3), pl.BlockSpec((1, d, de), hi3), pl.BlockSpec((1, de, d), hi3),
        ],
        out_specs=pl.BlockSpec(memory_space=pl.ANY),
        scratch_shapes=[
            pltpu.VMEM((2, tm, PACK_W), F32),
            pltpu.VMEM((2, tm, d), F32),
            pltpu.SemaphoreType.DMA((2,)),
            pltpu.SemaphoreType.DMA((2,)),
        ],
    )
    return pl.pallas_call(
        functools.partial(_moe_kernel, n_tokens=n_tokens),
        grid_spec=grid_spec,
        out_shape=jax.ShapeDtypeStruct((n_tokens + 2 * tm, d), F32),
        compiler_params=_params(("arbitrary",)),
        name="moe",
    )(src, elo, ehi, n_used, hp, wg, wu, wd, wg, wu, wd)


def _residual_kernel(x_ref, y_ref, mod_ref, o_ref):
    o_ref[0] = x_ref[0] + mod_ref[0, 5:6, :] * y_ref[...]


def _residual_norm_kernel(x_ref, y_ref, mod_ref, g_ref, o_ref):
    x = x_ref[0] + mod_ref[0, 5:6, :] * y_ref[...]
    o_ref[0] = x * lax.rsqrt(jnp.mean(x * x, axis=-1, keepdims=True) + EPS) * g_ref[...]


def _residual_call(x, y, mod, final_g=None):
    bsz, s, d = x.shape
    tm = ROW_TILE
    nt = s // tm
    in_specs = [
        pl.BlockSpec((1, tm, d), lambda b, i: (b, i, 0)),
        pl.BlockSpec((tm, d), lambda b, i: (b * nt + i, 0)),
        pl.BlockSpec((1, 6, d), lambda b, i: (b, 0, 0)),
    ]
    args = [x, y, mod]
    kern = _residual_kernel
    if final_g is not None:
        in_specs.append(_const_spec((1, d)))
        args.append(final_g)
        kern = _residual_norm_kernel
    return pl.pallas_call(
        kern,
        grid=(bsz, nt),
        in_specs=in_specs,
        out_specs=pl.BlockSpec((1, tm, d), lambda b, i: (b, i, 0)),
        out_shape=jax.ShapeDtypeStruct((bsz, s, d), F32),
        compiler_params=_params(("arbitrary", "arbitrary")),
        name="residual",
    )(*args)


def _kv_proj_kernel(x_ref, mod_ref, g_ref, wd_ref, lg_ref, wk_ref, wv_ref, cs_ref, sn_ref,
                    k_ref, v_ref):
    h = _norm_mod(x_ref[0], g_ref[...], mod_ref[0, 0:1, :], mod_ref[0, 1:2, :])
    dn = _dot(h.astype(BF16), wd_ref[...])
    ckv = dn[:, 0:KV_RANK]
    c = ckv * lax.rsqrt(jnp.mean(ckv * ckv, axis=-1, keepdims=True) + EPS) * lg_ref[...]
    cb = c.astype(BF16)
    kr = (dn[:, KV_RANK:KV_RANK + LANES] * cs_ref[0]
          + dn[:, KV_RANK + LANES:KV_RANK + 2 * LANES] * sn_ref[0])
    k_ref[0] = (_dot(cb, wk_ref[...]) + jnp.tile(kr, (1, N_HEADS))).astype(BF16)
    v_ref[0, 0] = _dot_nt(wv_ref[...], cb).astype(BF16)


def _kv_proj_call(x, mod, g, wd, lg, wk, wv, cs, sn):
    bsz, s, d = x.shape
    tm = ROW_TILE
    hp = N_HEADS * HEAD_PAD
    row = lambda w: pl.BlockSpec((1, tm, w), lambda b, i: (b, i, 0))
    return pl.pallas_call(
        _kv_proj_kernel,
        grid=(bsz, s // tm),
        in_specs=[
            row(d),
            pl.BlockSpec((1, 2, d), lambda b, i: (b, 0, 0)),
            _const_spec((1, d)),
            _const_spec(wd.shape), _const_spec(lg.shape), _const_spec(wk.shape), _const_spec(wv.shape),
            row(LANES), row(LANES),
        ],
        out_specs=[row(hp), pl.BlockSpec((1, 1, d, tm), lambda b, i: (b, i, 0, 0))],
        out_shape=[jax.ShapeDtypeStruct((bsz, s, hp), BF16),
                   jax.ShapeDtypeStruct((bsz, s // tm, d, tm), BF16)],
        compiler_params=_params(("arbitrary", "arbitrary")),
        name="kv_proj",
    )(x, mod, g, wd, lg, wk, wv, cs, sn)


def _q_proj_kernel(x_ref, mod_ref, g_ref, wdq_ref, qg_ref, wuq_ref, tab_ref, q_ref):
    h = _norm_mod(x_ref[0], g_ref[...], mod_ref[0, 0:1, :], mod_ref[0, 1:2, :])
    cq = _dot(h.astype(BF16), wdq_ref[...])
    cq = cq * lax.rsqrt(jnp.mean(cq * cq, axis=-1, keepdims=True) + EPS) * qg_ref[...]
    q = _dot(cq.astype(BF16), wuq_ref[...])
    q_ref[0] = (q * jnp.tile(tab_ref[0], (1, N_HEADS))).astype(BF16)


def _q_proj_call(x, mod, g, wdq, qg, wuq, tab):
    bsz, s, d = x.shape
    tm = ROW_TILE
    hp = N_HEADS * HEAD_PAD
    row = lambda w: pl.BlockSpec((1, tm, w), lambda b, i: (b, i, 0))
    return pl.pallas_call(
        _q_proj_kernel,
        grid=(bsz, s // tm),
        in_specs=[
            row(d),
            pl.BlockSpec((1, 6, d), lambda b, i: (b, 0, 0)),
            _const_spec((1, d)),
            _const_spec(wdq.shape), _const_spec(qg.shape), _const_spec(wuq.shape),
            row(LANES),
        ],
        out_specs=row(hp),
        out_shape=jax.ShapeDtypeStruct((bsz, s, hp), BF16),
        compiler_params=_params(("arbitrary", "arbitrary")),
        name="q_proj",
    )(x, mod, g, wdq, qg, wuq, tab)


def _pad_heads(w, width):
    k = w.shape[0]
    w = w.reshape(k, N_HEADS, width)
    w = jnp.pad(w, ((0, 0), (0, 0), (0, HEAD_PAD - width)))
    return w.reshape(k, N_HEADS * HEAD_PAD)


def _rot_cols(w):
    half = ROPE_DIM // 2
    return jnp.concatenate([-w[..., half:], w[..., :half]], axis=-1)


def _bias_placement():
    rows = jnp.arange(LANES)[:, None]
    cols = jnp.arange(N_HEADS * HEAD_PAD)[None, :]
    head = cols // HEAD_PAD
    off = cols % HEAD_PAD - HEAD_DIM
    part = jnp.where((off >= 0) & (off < 3), off, -1)
    is_val = (part >= 0) & (rows == part * N_HEADS + head)
    is_one = (off >= 3) & (off < 6) & (rows == 3 * N_HEADS)
    pq = jnp.where(is_val | is_one, 1.0, 0.0)
    part_k = jnp.where((off >= 3) & (off < 6), off - 3, -1)
    is_val_k = (part_k >= 0) & (rows == part_k * N_HEADS + head)
    is_one_k = (off >= 0) & (off < 3) & (rows == 3 * N_HEADS)
    pk = jnp.where(is_val_k, -1.0, 0.0) + jnp.where(is_one_k, 1.0, 0.0)
    return pq.astype(BF16), pk.astype(BF16)


def _moe_plan(rout, rank, counts, n_tokens):
    tm = MOE_TILE
    n_tiles = n_tokens // tm + N_CLASSES
    cls = rout[0].astype(I32)
    cnt = counts[:N_CLASSES, 0].astype(I32)
    tiles = (cnt + tm - 1) // tm
    tile_end = jnp.cumsum(tiles)
    tile_start = tile_end - tiles
    pos = tile_start[cls] * tm + rank[0].astype(I32)
    src = jnp.full((n_tiles * tm,), -1, I32).at[pos].set(jnp.arange(n_tokens, dtype=I32))
    n_used = tile_end[-1]
    tile_id = jnp.minimum(jnp.arange(n_tiles, dtype=I32), n_used - 1)
    tile_cls = jnp.sum((tile_end[None, :] <= tile_id[:, None]).astype(I32), axis=1)
    grp = tile_cls // N_PAIRS
    pair = tile_cls % N_PAIRS
    lo_of = jnp.array([0, 0, 0, 1, 1, 2], I32)
    hi_of = jnp.array([1, 2, 3, 2, 3, 3], I32)
    elo = grp * EXPERTS_PER_GROUP + lo_of[pair]
    ehi = grp * EXPERTS_PER_GROUP + hi_of[pair]
    return src, elo, ehi, n_used.reshape(1).astype(I32)


def _moe_layer(x1, hp, rout, mod, wg, wu, wd, triu, final_g=None):
    bsz, s, _ = x1.shape
    n = bsz * s
    rank, counts = _rank_call(rout[0:1], triu)
    src, elo, ehi, n_used = _moe_plan(rout, rank, counts, n)
    y = _moe_call(src, elo, ehi, n_used, hp, wg, wu, wd, n)
    return _residual_call(x1, y, mod, final_g)


def kernel(x, c, positions, a_norm_g, a_w_in, a_b_f, a_w_o, kv_norm_g, kv_w_mod, kv_b_mod, kv_w_down, kv_latent_g, kv_w_up, b_norm_g, b_w_dq, b_q_norm_g, b_w_uq, b_w_o, w_mod, b_mod, ffn_norm_g, router_w, router_bias, exp_w_gate, exp_w_up, exp_w_down, final_norm_g):
    bsz, s, d = x.shape
    n = bsz * s

    mod0 = _mod_call(c, w_mod[0], b_mod[0]).reshape(bsz, 6, d)
    mod1 = _mod_call(c, w_mod[1], b_mod[1]).reshape(bsz, 6, d)
    modkv = _mod_call(c, kv_w_mod, kv_b_mod).reshape(bsz, 2, d)

    w_in = a_w_in[0]
    wq = _pad_heads(w_in[:, 0:d], HEAD_DIM).astype(BF16)
    wk = _pad_heads(w_in[:, d:2 * d], HEAD_DIM).astype(BF16)
    wv = w_in[:, 2 * d:3 * d].T.astype(BF16)
    wf16 = w_in[:, 3 * d:]
    wf = jnp.pad(jnp.concatenate([wf16, wf16, wf16], axis=1), ((0, 0), (0, LANES - 3 * N_HEADS))).astype(BF16)
    bf = jnp.pad(jnp.concatenate([a_b_f[0]] * 3), (0, LANES - 3 * N_HEADS)).reshape(1, LANES)
    tril = jnp.tril(jnp.ones((ROW_TILE, ROW_TILE), F32)).astype(BF16)
    triu = jnp.triu(jnp.ones((RANK_TILE, RANK_TILE), F32)).astype(BF16)
    pq, pk = _bias_placement()

    rw_t = router_w.T
    rwh = rw_t.astype(BF16)
    rwl = (rw_t - rwh.astype(F32)).astype(BF16)
    rb = jnp.broadcast_to(router_bias.astype(F32)[:, None], (N_EXPERTS, ROW_TILE))

    ewg = exp_w_gate.astype(BF16)
    ewu = exp_w_up.astype(BF16)
    ewd = exp_w_down.astype(BF16)

    half = ROPE_DIM // 2
    inv_freq = ROPE_THETA ** (-jnp.arange(half, dtype=F32) / half)
    ang = positions.astype(F32)[..., None] * inv_freq
    cos2 = jnp.concatenate([jnp.cos(ang)] * 2, axis=-1)
    sin2 = jnp.concatenate([jnp.sin(ang)] * 2, axis=-1)
    zeros64 = jnp.zeros((bsz, s, NOPE_DIM), F32)
    qscale = (NOPE_DIM + ROPE_DIM) ** -0.5
    q_tab = jnp.concatenate([jnp.ones_like(zeros64), cos2, sin2], axis=-1) * qscale
    k_cos = jnp.concatenate([zeros64, cos2, cos2], axis=-1)
    k_sin = jnp.concatenate([zeros64, sin2, sin2], axis=-1)

    w_ckv = kv_w_down[:, :KV_RANK]
    w_kr = kv_w_down[:, KV_RANK:]
    zcol = jnp.zeros((d, NOPE_DIM), F32)
    w_down = jnp.concatenate([w_ckv, zcol, w_kr, w_kr, zcol, _rot_cols(w_kr), _rot_cols(w_kr)], axis=1).astype(BF16)
    w_up3 = kv_w_up.reshape(KV_RANK, N_HEADS, NOPE_DIM + V_DIM)
    w_upk = _pad_heads(w_up3[:, :, :NOPE_DIM].reshape(KV_RANK, N_HEADS * NOPE_DIM), NOPE_DIM).astype(BF16)
    w_upv = w_up3[:, :, NOPE_DIM:].reshape(KV_RANK, N_HEADS * V_DIM).T.astype(BF16)

    w_uq3 = b_w_uq[0].reshape(Q_RANK, N_HEADS, NOPE_DIM + ROPE_DIM)
    w_uq = jnp.concatenate([w_uq3, _rot_cols(w_uq3[:, :, NOPE_DIM:])], axis=-1)
    w_uq = w_uq.reshape(Q_RANK, N_HEADS * HEAD_PAD).astype(BF16)

    qa, ka, va = _fox_proj_call(x, mod0, a_norm_g[0].reshape(1, d), wf, bf, tril, pq, pk, wq, wk, wv)
    attn = _attn_call(qa, ka, va, chunk_mask=False)
    x1, hp, rout = _out_router_call(attn, a_w_o[0].astype(BF16), x, mod0, ffn_norm_g[0].reshape(1, d),
                                    rwh, rwl, rb)
    x2 = _moe_layer(x1, hp, rout, mod0, ewg[0], ewu[0], ewd[0], triu)

    kb, vb = _kv_proj_call(x2, modkv, kv_norm_g.reshape(1, d), w_down, kv_latent_g.reshape(1, KV_RANK),
                           w_upk, w_upv, k_cos, k_sin)

    qb = _q_proj_call(x2, mod1, b_norm_g[0].reshape(1, d), b_w_dq[0].astype(BF16),
                      b_q_norm_g[0].reshape(1, Q_RANK), w_uq, q_tab)
    attn = _attn_call(qb, kb, vb, chunk_mask=True)
    x3, hp, rout = _out_router_call(attn, b_w_o[0].astype(BF16), x2, mod1, ffn_norm_g[1].reshape(1, d),
                                    rwh, rwl, rb)
    return _moe_layer(x3, hp, rout, mod1, ewg[1], ewu[1], ewd[1], triu, final_norm_g.reshape(1, d))
```
